```python
import math
import jax, jax.numpy as jnp
from jax import lax
import numpy as np

D_MODEL = 1024
BATCH = 4
SEQ = 4096
DEPTH = 2
DEC_BATCH = 128
DEC_SEQ = 4
PAST_LEN = 8192
PAGE_SIZE = 128

MLA_HEADS = 8
MLA_Q_LORA = 256
MLA_KV_LORA = 128
MLA_NOPE = 64
MLA_ROPE = 32
MLA_V = 64
ROPE_THETA = 10000.0
DSA_HEADS = 8
DSA_KV_HEADS = 2
DSA_HEAD_DIM = 64
IDX_HEADS = 4
IDX_DIM = 64
IDX_TOPK = 256
SSD_HEADS = 8
SSD_HEAD_DIM = 64
SSD_INNER = SSD_HEADS * SSD_HEAD_DIM
SSD_GROUPS = 2
SSD_STATE = 64
SSD_CONV = 4
SSD_CHUNK = 128
SSD_CONV_DIM = SSD_INNER + 2 * SSD_GROUPS * SSD_STATE
DT_MIN = 0.001
DT_MAX = 0.1
N_BRANCH = 3
BRANCH_W = 512
REL_BUCKETS = 32
REL_MAX_EXACT = 16
REL_MAX_DIST = 128
D_FF = 2816
N_EXPERTS = 8
TOP_K_EXPERTS = 2
D_FF_EXPERT = D_FF // 2
N_DENSE = (DEPTH + 1) // 2
N_MOE = DEPTH // 2
Q_BLOCK = 128
NORM_EPS = 1e-6
IN_SIZES = (MLA_Q_LORA, MLA_KV_LORA, MLA_ROPE,
            DSA_HEADS * DSA_HEAD_DIM, DSA_KV_HEADS * DSA_HEAD_DIM, DSA_KV_HEADS * DSA_HEAD_DIM,
            IDX_HEADS * IDX_DIM, IDX_DIM, IDX_HEADS,
            SSD_INNER, SSD_CONV_DIM, SSD_HEADS,
            N_BRANCH * D_MODEL)
IN_COLS = sum(IN_SIZES)

kernel_name = 'hybrid_mla_dsa_ssd_gated_decoder_step'


def rmsnorm(x, w):
    xf = x.astype(jnp.float32)
    y = xf * lax.rsqrt(jnp.mean(xf * xf, axis=-1, keepdims=True) + NORM_EPS)
    return (y * w.astype(jnp.float32)).astype(x.dtype)


def rope(x, pos):
    half = x.shape[-1] // 2
    freqs = ROPE_THETA ** (-jnp.arange(half, dtype=jnp.float32) / half)
    ang = pos.astype(jnp.float32)[:, None] * freqs
    shape = (pos.shape[0],) + (1,) * (x.ndim - 3) + (half,)
    cos = jnp.cos(ang).reshape(shape)
    sin = jnp.sin(ang).reshape(shape)
    xf = x.astype(jnp.float32)
    x1, x2 = xf[..., :half], xf[..., half:]
    return jnp.concatenate([x1 * cos - x2 * sin, x1 * sin + x2 * cos], axis=-1).astype(x.dtype)


def rel_bucket(dist):
    n = jnp.maximum(dist, 0)
    nf = jnp.maximum(n, 1).astype(jnp.float32)
    large = REL_MAX_EXACT + (jnp.log(nf / REL_MAX_EXACT) / math.log(REL_MAX_DIST / REL_MAX_EXACT)
                             * (REL_BUCKETS - REL_MAX_EXACT)).astype(jnp.int32)
    large = jnp.minimum(large, REL_BUCKETS - 1)
    return jnp.where(n < REL_MAX_EXACT, n, large)


def take_rows(a, idx):
    return jax.vmap(lambda ab, ib: ab[ib])(a, idx)


def gather_pages(pool, layer, page_table):
    g = pool[layer, page_table]
    return g.reshape((page_table.shape[0], page_table.shape[1] * pool.shape[2]) + pool.shape[3:])


def map_query_blocks(fn, q_pos, *qs):
    L = q_pos.shape[0]
    if L > Q_BLOCK and L % Q_BLOCK == 0:
        nb = L // Q_BLOCK
        def split(a):
            return jnp.moveaxis(a.reshape((a.shape[0], nb, Q_BLOCK) + a.shape[2:]), 1, 0)
        out = lax.map(lambda args: fn(*args), (q_pos.reshape(nb, Q_BLOCK),) + tuple(split(a) for a in qs))
        out = jnp.moveaxis(out, 0, 1)
        return out.reshape((out.shape[0], L) + out.shape[3:])
    return fn(q_pos, *qs)


def mla_attention(q_lat, q_rope, q_pos, ckv, krope):
    k_pos = jnp.arange(ckv.shape[1], dtype=jnp.int32)
    scale = (MLA_NOPE + MLA_ROPE) ** -0.5
    def block(qp, ql, qr):
        logits = (jnp.einsum('blhc,bsc->bhls', ql, ckv)
                  + jnp.einsum('blhr,bsr->bhls', qr, krope)).astype(jnp.float32) * scale
        logits = jnp.where(qp[:, None] >= k_pos[None, :], logits, -jnp.inf)
        p = jax.nn.softmax(logits, axis=-1).astype(ckv.dtype)
        return jnp.einsum('bhls,bsc->blhc', p, ckv)
    return map_query_blocks(block, q_pos, q_lat, q_rope)


def dsa_attention(q, qi, wi, q_pos, ki, fetch, rel_bias, n_sel):
    k_pos = jnp.arange(ki.shape[1], dtype=jnp.int32)
    grp = DSA_HEADS // DSA_KV_HEADS
    def block(qp, qb, qib, wib):
        nb_, lb = qb.shape[:2]
        dots = jnp.einsum('blhd,bsd->blhs', qib, ki).astype(jnp.float32) * IDX_DIM ** -0.5
        score = jnp.einsum('blh,blhs->bls', wib.astype(jnp.float32), jax.nn.relu(dots))
        score = jnp.where(qp[:, None] >= k_pos[None, :], score, -jnp.inf)
        _, sel = lax.top_k(score, n_sel)
        k_sel, v_sel = fetch(sel)
        dist = qp[None, :, None] - sel
        bias = rel_bias[rel_bucket(dist)].reshape(nb_, lb, n_sel, DSA_KV_HEADS, grp)
        bias = jnp.moveaxis(bias, 2, 4)
        qg = qb.reshape(nb_, lb, DSA_KV_HEADS, grp, DSA_HEAD_DIM)
        logits = (jnp.einsum('blkgd,blnkd->blkgn', qg, k_sel).astype(jnp.float32) * DSA_HEAD_DIM ** -0.5
                  + bias.astype(jnp.float32))
        logits = jnp.where((dist >= 0)[:, :, None, None, :], logits, -jnp.inf)
        p = jax.nn.softmax(logits, axis=-1).astype(v_sel.dtype)
        o = jnp.einsum('blkgn,blnkd->blkgd', p, v_sel)
        return o.reshape(nb_, lb, DSA_HEADS, DSA_HEAD_DIM)
    return map_query_blocks(block, q_pos, q, qi, wi)


def segsum(a):
    t = a.shape[-1]
    cs = jnp.cumsum(a, axis=-1)
    d = cs[..., :, None] - cs[..., None, :]
    return jnp.where(jnp.tril(jnp.ones((t, t), dtype=bool)), d, -jnp.inf)


def ssd_scan(x, dt, a, bm, cm, init_state):
    b, l, h, p = x.shape
    n = bm.shape[-1]
    q = math.gcd(l, SSD_CHUNK)
    c = l // q
    xc = (x * dt[..., None]).reshape(b, c, q, h, p)
    bc = bm.reshape(b, c, q, h, n)
    cc = cm.reshape(b, c, q, h, n)
    ad = jnp.transpose((dt * a).reshape(b, c, q, h), (0, 3, 1, 2))
    a_cum = jnp.cumsum(ad, axis=-1)
    cb = jnp.einsum('bclhn,bcshn->bhcls', cc, bc) * jnp.exp(segsum(ad))
    y_diag = jnp.einsum('bhcls,bcshp->bclhp', cb, xc)
    decay_states = jnp.exp(a_cum[..., -1:] - a_cum)
    states = jnp.einsum('bclhn,bhcl,bclhp->bchpn', bc, decay_states, xc)
    states = jnp.concatenate([init_state[:, None], states], axis=1)
    chunk_a = jnp.pad(a_cum[..., -1], ((0, 0), (0, 0), (1, 0)))
    decay_chunk = jnp.exp(segsum(chunk_a))
    new_states = jnp.einsum('bhzc,bchpn->bzhpn', decay_chunk, states)
    states_in, final = new_states[:, :-1], new_states[:, -1]
    y_off = jnp.einsum('bclhn,bchpn,bhcl->bclhp', cc, states_in, jnp.exp(a_cum))
    return (y_diag + y_off).reshape(b, l, h, p), final


def ssd_mixer(z, xbc, dt_raw, conv_prev, ssm_prev, lp):
    bsz, L, _ = xbc.shape
    xp = jnp.concatenate([conv_prev.astype(xbc.dtype), xbc], axis=1)
    conv = lp['ssd_conv_b']
    for k in range(SSD_CONV):
        conv = conv + xp[:, k:k + L] * lp['ssd_conv_w'][k]
    u = jax.nn.silu(conv)
    xs = u[..., :SSD_INNER].reshape(bsz, L, SSD_HEADS, SSD_HEAD_DIM)
    hpg = SSD_HEADS // SSD_GROUPS
    bm = jnp.repeat(u[..., SSD_INNER:SSD_INNER + SSD_GROUPS * SSD_STATE].reshape(bsz, L, SSD_GROUPS, SSD_STATE), hpg, axis=2)
    cm = jnp.repeat(u[..., SSD_INNER + SSD_GROUPS * SSD_STATE:].reshape(bsz, L, SSD_GROUPS, SSD_STATE), hpg, axis=2)
    dt = jax.nn.softplus(dt_raw.astype(jnp.float32) + lp['ssd_dt_bias'].astype(jnp.float32))
    a = -jnp.exp(lp['ssd_a_log'].astype(jnp.float32))
    xf = xs.astype(jnp.float32)
    y, final = ssd_scan(xf, dt, a, bm.astype(jnp.float32), cm.astype(jnp.float32), ssm_prev.astype(jnp.float32))
    y = y + lp['ssd_d'].astype(jnp.float32)[:, None] * xf
    y = y.reshape(bsz, L, SSD_INNER).astype(z.dtype)
    y = rmsnorm(y * jax.nn.silu(z), lp['ssd_norm'])
    return y, xp[:, -(SSD_CONV - 1):], final.astype(ssm_prev.dtype)


def swiglu(x, w_gate, w_up, w_down):
    return (jax.nn.silu(x @ w_gate) * (x @ w_up)) @ w_down


def moe_swiglu(x, router_w, router_b, w_gate, w_up, w_down):
    logits = (x @ router_w).astype(jnp.float32) + router_b.astype(jnp.float32)
    top_v, top_i = lax.top_k(logits, TOP_K_EXPERTS)
    top_p = jax.nn.softmax(top_v, axis=-1)
    gate = jnp.sum(jax.nn.one_hot(top_i, N_EXPERTS, dtype=jnp.float32) * top_p[..., None], axis=-2).astype(x.dtype)
    out = jnp.zeros_like(x)
    for e in range(N_EXPERTS):
        out = out + gate[..., e:e + 1] * swiglu(x, w_gate[e], w_up[e], w_down[e])
    return out


def token_mixers(h, past_len, past, conv_prev, ssm_prev, lp, rel_bias):
    bsz, L, _ = h.shape
    q_pos = past_len + jnp.arange(L, dtype=jnp.int32)
    proj = h @ lp['w_in']
    cuts = np.cumsum(IN_SIZES)[:-1].tolist()
    (cq, ckv_raw, kr_raw, q_d, k_d, v_d, qi, ki_new, wi, z, xbc, dt_raw, gate_raw) = jnp.split(proj, cuts, axis=-1)

    cq = rmsnorm(cq, lp['mla_q_norm'])
    qa = (cq @ lp['mla_w_q_up']).reshape(bsz, L, MLA_HEADS, MLA_NOPE + MLA_ROPE)
    q_rope = rope(qa[..., MLA_NOPE:], q_pos)
    w_kv = lp['mla_w_kv_up'].reshape(MLA_KV_LORA, MLA_HEADS, MLA_NOPE + MLA_V)
    q_lat = jnp.einsum('blhd,chd->blhc', qa[..., :MLA_NOPE], w_kv[..., :MLA_NOPE])
    ckv_new = rmsnorm(ckv_raw, lp['mla_kv_norm'])
    kr_new = rope(kr_raw, q_pos)

    q_d = q_d.reshape(bsz, L, DSA_HEADS, DSA_HEAD_DIM)
    k_d = k_d.reshape(bsz, L, DSA_KV_HEADS, DSA_HEAD_DIM)
    v_d = v_d.reshape(bsz, L, DSA_KV_HEADS, DSA_HEAD_DIM)
    qi = qi.reshape(bsz, L, IDX_HEADS, IDX_DIM)
    wi = wi * IDX_HEADS ** -0.5

    if past is None:
        ckv_all, kr_all, ki_all = ckv_new, kr_new, ki_new
        def fetch(sel):
            return take_rows(k_d, sel), take_rows(v_d, sel)
    else:
        layer, pt = past['layer'], past['page_table']
        ckv_all = jnp.concatenate([gather_pages(past['ckv'], layer, pt).astype(ckv_new.dtype), ckv_new], axis=1)
        kr_all = jnp.concatenate([gather_pages(past['krope'], layer, pt).astype(kr_new.dtype), kr_new], axis=1)
        ki_all = jnp.concatenate([gather_pages(past['ki'], layer, pt).astype(ki_new.dtype), ki_new], axis=1)
        k_pool, v_pool = past['k'], past['v']
        def fetch(sel):
            in_past = (sel < past_len)[..., None, None]
            sp = jnp.minimum(sel, past_len - 1)
            phys = take_rows(pt, sp // PAGE_SIZE)
            off = sp % PAGE_SIZE
            sn = jnp.maximum(sel - past_len, 0)
            k_sel = jnp.where(in_past, k_pool[layer, phys, off].astype(k_d.dtype), take_rows(k_d, sn))
            v_sel = jnp.where(in_past, v_pool[layer, phys, off].astype(v_d.dtype), take_rows(v_d, sn))
            return k_sel, v_sel

    o_lat = mla_attention(q_lat, q_rope, q_pos, ckv_all, kr_all)
    o_a = jnp.einsum('blhc,chv->blhv', o_lat, w_kv[..., MLA_NOPE:]).reshape(bsz, L, BRANCH_W)

    n_keys = past_len + L
    n_sel = max(1, min(IDX_TOPK, n_keys // 4))
    o_b = dsa_attention(q_d, qi, wi, q_pos, ki_all, fetch, rel_bias, n_sel).reshape(bsz, L, BRANCH_W)

    o_c, conv_new, ssm_new = ssd_mixer(z, xbc, dt_raw, conv_prev, ssm_prev, lp)

    gates = jax.nn.sigmoid(gate_raw.reshape(bsz, L, N_BRANCH, D_MODEL) + lp['gate_b'])
    branches = jnp.stack([o_a, o_b, o_c], axis=2)
    up = jnp.einsum('blnw,nwd->blnd', branches, lp['w_branch'])
    out = jnp.einsum('blnd,blnd->bld', gates, up) @ lp['w_out']
    return out, (ckv_new, kr_new, k_d, v_d, ki_new, ssm_new, conv_new)


def run_trunk(x, past_len, pasts, conv_prevs, ssm_prevs, layers, rel_bias, final_norm_w):
    news = []
    for l in range(DEPTH):
        lp = layers[l]
        h = rmsnorm(x, lp['norm_mix'])
        mix, new = token_mixers(h, past_len, pasts[l], conv_prevs[l], ssm_prevs[l], lp, rel_bias)
        x = x + mix
        h = rmsnorm(x, lp['norm_ffn'])
        x = x + (swiglu(h, *lp['ffn']) if l % 2 == 0 else moe_swiglu(h, *lp['ffn']))
        news.append(new)
    y = rmsnorm(x, final_norm_w)
    return y, [jnp.stack([nw[i] for nw in news]) for i in range(7)]


def setup_inputs(seed: int = 0) -> dict:
    key = jax.random.key(seed)
    it = iter(jax.random.split(key, 48))
    f32 = jnp.float32
    D = D_MODEL
    def nrm(shape, scale):
        return jax.random.normal(next(it), shape, f32) * scale
    def gain(shape):
        return 1.0 + nrm(shape, 0.02)
    n_pages = PAST_LEN // PAGE_SIZE
    n_used = DEC_BATCH * n_pages
    n_pool = n_used + max(1, n_used // 4)
    x_prompt = nrm((BATCH, SEQ, D), 1.0)
    x_sample = nrm((DEC_BATCH, DEC_SEQ, D), 1.0)
    cache_mla_ckv = nrm((DEPTH, n_pool, PAGE_SIZE, MLA_KV_LORA), 1.0)
    cache_mla_krope = nrm((DEPTH, n_pool, PAGE_SIZE, MLA_ROPE), 1.0)
    cache_dsa_k = nrm((DEPTH, n_pool, PAGE_SIZE, DSA_KV_HEADS, DSA_HEAD_DIM), 1.0)
    cache_dsa_v = nrm((DEPTH, n_pool, PAGE_SIZE, DSA_KV_HEADS, DSA_HEAD_DIM), 1.0)
    cache_dsa_idx_k = nrm((DEPTH, n_pool, PAGE_SIZE, IDX_DIM), 1.0)
    state_ssm = nrm((DEPTH, DEC_BATCH, SSD_HEADS, SSD_HEAD_DIM, SSD_STATE), 0.5)
    state_conv = nrm((DEPTH, DEC_BATCH, SSD_CONV - 1, SSD_CONV_DIM), 1.0)
    page_table = jax.random.permutation(next(it), n_pool)[:n_used].reshape(DEC_BATCH, n_pages).astype(jnp.int32)
    rel_bias = nrm((REL_BUCKETS, DSA_HEADS), 0.5)
    norm_mix_w = gain((DEPTH, D))
    w_in = nrm((DEPTH, D, IN_COLS), D ** -0.5)
    mla_q_norm_w = gain((DEPTH, MLA_Q_LORA))
    mla_w_q_up = nrm((DEPTH, MLA_Q_LORA, MLA_HEADS * (MLA_NOPE + MLA_ROPE)), MLA_Q_LORA ** -0.5)
    mla_kv_norm_w = gain((DEPTH, MLA_KV_LORA))
    mla_w_kv_up = nrm((DEPTH, MLA_KV_LORA, MLA_HEADS * (MLA_NOPE + MLA_V)), MLA_KV_LORA ** -0.5)
    ssd_conv_w = nrm((DEPTH, SSD_CONV, SSD_CONV_DIM), SSD_CONV ** -0.5)
    ssd_conv_b = nrm((DEPTH, SSD_CONV_DIM), 0.02)
    dt = jnp.exp(jax.random.uniform(next(it), (DEPTH, SSD_HEADS), f32)
                 * (math.log(DT_MAX) - math.log(DT_MIN)) + math.log(DT_MIN))
    ssd_dt_bias = dt + jnp.log(-jnp.expm1(-dt))
    ssd_a_log = jnp.log(jax.random.uniform(next(it), (DEPTH, SSD_HEADS), f32, 1.0, 16.0))
    ssd_d = 1.0 + nrm((DEPTH, SSD_HEADS), 0.1)
    ssd_norm_w = gain((DEPTH, SSD_INNER))
    gate_b = nrm((DEPTH, N_BRANCH, D), 0.02)
    w_branch = nrm((DEPTH, N_BRANCH, BRANCH_W, D), BRANCH_W ** -0.5)
    w_out = nrm((DEPTH, D, D), D ** -0.5)
    norm_ffn_w = gain((DEPTH, D))
    ffn_w_gate = nrm((N_DENSE, D, D_FF), D ** -0.5)
    ffn_w_up = nrm((N_DENSE, D, D_FF), D ** -0.5)
    ffn_w_down = nrm((N_DENSE, D_FF, D), D_FF ** -0.5)
    moe_router_w = nrm((N_MOE, D, N_EXPERTS), D ** -0.5)
    moe_router_b = nrm((N_MOE, N_EXPERTS), 0.01)
    moe_w_gate = nrm((N_MOE, N_EXPERTS, D, D_FF_EXPERT), D ** -0.5)
    moe_w_up = nrm((N_MOE, N_EXPERTS, D, D_FF_EXPERT), D ** -0.5)
    moe_w_down = nrm((N_MOE, N_EXPERTS, D_FF_EXPERT, D), D_FF_EXPERT ** -0.5)
    final_norm_w = gain((D,))
    return {
        'x_prompt': x_prompt, 'x_sample': x_sample,
        'cache_mla_ckv': cache_mla_ckv, 'cache_mla_krope': cache_mla_krope,
        'cache_dsa_k': cache_dsa_k, 'cache_dsa_v': cache_dsa_v, 'cache_dsa_idx_k': cache_dsa_idx_k,
        'state_ssm': state_ssm, 'state_conv': state_conv, 'page_table': page_table,
        'rel_bias': rel_bias, 'norm_mix_w': norm_mix_w, 'w_in': w_in,
        'mla_q_norm_w': mla_q_norm_w, 'mla_w_q_up': mla_w_q_up,
        'mla_kv_norm_w': mla_kv_norm_w, 'mla_w_kv_up': mla_w_kv_up,
        'ssd_conv_w': ssd_conv_w, 'ssd_conv_b': ssd_conv_b, 'ssd_dt_bias': ssd_dt_bias,
        'ssd_a_log': ssd_a_log, 'ssd_d': ssd_d, 'ssd_norm_w': ssd_norm_w,
        'gate_b': gate_b, 'w_branch': w_branch, 'w_out': w_out, 'norm_ffn_w': norm_ffn_w,
        'ffn_w_gate': ffn_w_gate, 'ffn_w_up': ffn_w_up, 'ffn_w_down': ffn_w_down,
        'moe_router_w': moe_router_w, 'moe_router_b': moe_router_b,
        'moe_w_gate': moe_w_gate, 'moe_w_up': moe_w_up, 'moe_w_down': moe_w_down,
        'final_norm_w': final_norm_w,
    }


def reference(x_prompt, x_sample, cache_mla_ckv, cache_mla_krope, cache_dsa_k, cache_dsa_v,
              cache_dsa_idx_k, state_ssm, state_conv, page_table, rel_bias, norm_mix_w, w_in,
              mla_q_norm_w, mla_w_q_up, mla_kv_norm_w, mla_w_kv_up, ssd_conv_w, ssd_conv_b,
              ssd_dt_bias, ssd_a_log, ssd_d, ssd_norm_w, gate_b, w_branch, w_out, norm_ffn_w,
              ffn_w_gate, ffn_w_up, ffn_w_down, moe_router_w, moe_router_b, moe_w_gate, moe_w_up,
              moe_w_down, final_norm_w):
    layers = []
    for l in range(DEPTH):
        j = l // 2
        if l % 2 == 0:
            ffn = (ffn_w_gate[j], ffn_w_up[j], ffn_w_down[j])
        else:
            ffn = (moe_router_w[j], moe_router_b[j], moe_w_gate[j], moe_w_up[j], moe_w_down[j])
        layers.append({
            'norm_mix': norm_mix_w[l], 'w_in': w_in[l],
            'mla_q_norm': mla_q_norm_w[l], 'mla_w_q_up': mla_w_q_up[l],
            'mla_kv_norm': mla_kv_norm_w[l], 'mla_w_kv_up': mla_w_kv_up[l],
            'ssd_conv_w': ssd_conv_w[l], 'ssd_conv_b': ssd_conv_b[l], 'ssd_dt_bias': ssd_dt_bias[l],
            'ssd_a_log': ssd_a_log[l], 'ssd_d': ssd_d[l], 'ssd_norm': ssd_norm_w[l],
            'gate_b': gate_b[l], 'w_branch': w_branch[l], 'w_out': w_out[l],
            'norm_ffn': norm_ffn_w[l], 'ffn': ffn,
        })

    bp = x_prompt.shape[0]
    zero_conv = jnp.zeros((bp, SSD_CONV - 1, SSD_CONV_DIM), x_prompt.dtype)
    zero_ssm = jnp.zeros((bp, SSD_HEADS, SSD_HEAD_DIM, SSD_STATE), x_prompt.dtype)
    y_prompt, new_p = run_trunk(x_prompt, 0, [None] * DEPTH, [zero_conv] * DEPTH, [zero_ssm] * DEPTH,
                                layers, rel_bias, final_norm_w)

    past_len = page_table.shape[1] * cache_mla_ckv.shape[2]
    pasts = [{'layer': l, 'page_table': page_table, 'ckv': cache_mla_ckv, 'krope': cache_mla_krope,
              'ki': cache_dsa_idx_k, 'k': cache_dsa_k, 'v': cache_dsa_v} for l in range(DEPTH)]
    y_sample, new_s = run_trunk(x_sample, past_len, pasts, [state_conv[l] for l in range(DEPTH)],
                                [state_ssm[l] for l in range(DEPTH)], layers, rel_bias, final_norm_w)

    p_ckv, p_krope, p_k, p_v, p_idx_k, p_ssm, p_conv = new_p
    s_ckv, s_krope, s_k, s_v, s_idx_k, s_ssm, s_conv = new_s
    return (y_prompt, y_sample, p_ckv, p_krope, p_k, p_v, p_idx_k, p_ssm, p_conv,
            s_ckv, s_krope, s_k, s_v, s_idx_k, s_ssm, s_conv)
```

```python
import functools
import math

import jax
import jax.numpy as jnp
import numpy as np
from jax import lax
from jax.experimental import pallas as pl
from jax.experimental.pallas import tpu as pltpu

F32 = jnp.float32
BF16 = jnp.bfloat16
I32 = jnp.int32

LANES = 128
BLK = 128
VMEM_LIMIT = 56 * 1024 * 1024

NORM_EPS = 1e-6
NEG = -1e30
INT_MIN = -2 ** 31

MLA_HEADS = 8
MLA_Q_LORA = 256
MLA_KV_LORA = 128
MLA_NOPE = 64
MLA_ROPE = 32
MLA_V = 64
ROPE_THETA = 10000.0
DSA_HEADS = 8
DSA_KV_HEADS = 2
DSA_HEAD_DIM = 64
IDX_HEADS = 4
IDX_DIM = 64
IDX_TOPK = 256
SSD_HEADS = 8
SSD_HEAD_DIM = 64
SSD_INNER = SSD_HEADS * SSD_HEAD_DIM
SSD_GROUPS = 2
SSD_STATE = 64
SSD_CONV = 4
SSD_CONV_DIM = SSD_INNER + 2 * SSD_GROUPS * SSD_STATE
N_BRANCH = 3
BRANCH_W = 512
REL_BUCKETS = 32
REL_MAX_EXACT = 16
REL_MAX_DIST = 128
TOP_K_EXPERTS = 2
QPAD = 8
DEC_PAGES_PER_STEP = 16

NT = (((1,), (1,)), ((), ()))
TN = (((0,), (0,)), ((), ()))

IN_GROUPS = (("gate", 3072), ("z", 512), ("xbc", 768), ("qd", 1024), ("cq", 256), ("qi", 512),
             ("kvc", 256), ("kvcp", 128), ("kd", 128), ("vd", 128), ("ki", 128), ("wi", 128), ("dt", 128))
IN_OFF = {}
_o = 0
for _n, _w in IN_GROUPS:
    IN_OFF[_n] = (_o, _w)
    _o += _w
IN_TOTAL = _o


def _cparams(sem):
    return pltpu.CompilerParams(dimension_semantics=sem, vmem_limit_bytes=VMEM_LIMIT)


def _rms(x, w):
    return x * lax.rsqrt(jnp.mean(x * x, axis=-1, keepdims=True) + NORM_EPS) * w


def _sigmoid(x):
    return 1.0 / (1.0 + jnp.exp(-x))


def _to_key(s):
    b = lax.bitcast_convert_type(s, I32)
    return jnp.where(b < 0, b ^ jnp.int32(0x7FFFFFFF), b)


def _in_proj_kernel(x_ref, nw_ref, w_ref, cos_ref, sin_ref, qnw_ref, kvnw_ref,
                    gate_o, z_o, xbc_o, qd_o, cqn_o, qi_o, kvc_o, ckv_o, kr_o, kd_o, vd_o, ki_o, wi_o, dt_o):
    h = _rms(x_ref[...], nw_ref[...]).astype(BF16)

    def mm(name):
        lo, n = IN_OFF[name]
        return jnp.dot(h, w_ref[:, lo:lo + n], preferred_element_type=F32)

    gate_o[...] = mm("gate")
    z_o[...] = mm("z")
    xbc_o[...] = mm("xbc")
    qd_o[...] = mm("qd").astype(BF16)
    cqn_o[...] = _rms(mm("cq"), qnw_ref[...]).astype(BF16)
    qi_o[...] = mm("qi").astype(BF16)
    kvc = mm("kvc")
    ckv = _rms(kvc[:, :MLA_KV_LORA], kvnw_ref[...])
    kr = kvc[:, MLA_KV_LORA:] * cos_ref[...] + mm("kvcp") * sin_ref[...]
    ckv_o[...] = ckv
    kr_o[...] = kr[:, :MLA_ROPE]
    kvc_o[...] = jnp.concatenate([ckv, kr], axis=1).astype(BF16)
    kd_o[...] = mm("kd")
    vd_o[...] = mm("vd")
    ki_o[...] = mm("ki")[:, :IDX_DIM]
    wi_o[...] = mm("wi")[:, :IDX_HEADS] * (IDX_HEADS ** -0.5)
    dt_o[...] = mm("dt")[:, :SSD_HEADS]


def _in_proj(x, nw, w_cat, cos_t, sin_t, qnw, kvnw, tm=256):
    T, D = x.shape
    widths = [(3072, F32), (512, F32), (768, F32), (1024, BF16), (256, BF16), (512, BF16), (256, BF16),
              (128, F32), (32, F32), (128, F32), (128, F32), (64, F32), (4, F32), (8, F32)]
    row = lambda n: pl.BlockSpec((tm, n), lambda i: (i, 0))
    full = lambda a: pl.BlockSpec(a.shape, lambda i: (0,) * a.ndim)
    return pl.pallas_call(
        _in_proj_kernel,
        grid=(T // tm,),
        in_specs=[row(D), full(nw), full(w_cat), row(LANES), row(LANES), full(qnw), full(kvnw)],
        out_specs=[row(n) for n, _ in widths],
        out_shape=[jax.ShapeDtypeStruct((T, n), dt) for n, dt in widths],
        compiler_params=_cparams(("parallel",)),
        name="in_proj",
    )(x, nw, w_cat, cos_t, sin_t, qnw, kvnw)


def _fold_kernel(a_ref, b_ref, o_ref):
    o_ref[...] = lax.dot_general(a_ref[...], b_ref[...], NT, precision=lax.Precision.HIGHEST,
                                 preferred_element_type=F32)


def _fold_q_lat(wq_nope, wk):
    H = wq_nope.shape[0]
    return pl.pallas_call(
        _fold_kernel,
        grid=(H,),
        in_specs=[pl.BlockSpec((None,) + wq_nope.shape[1:], lambda h: (h, 0, 0)),
                  pl.BlockSpec((None,) + wk.shape[1:], lambda h: (h, 0, 0))],
        out_specs=pl.BlockSpec((None, wq_nope.shape[1], wk.shape[1]), lambda h: (h, 0, 0)),
        out_shape=jax.ShapeDtypeStruct((H, wq_nope.shape[1], wk.shape[1]), F32),
        compiler_params=_cparams(("parallel",)),
        name="fold_q_lat",
    )(wq_nope, wk)


def _q_proj_kernel(c_ref, w_ref, wp_ref, cos_ref, sin_ref, o_ref):
    c = c_ref[...]
    a = jnp.dot(c, w_ref[...], preferred_element_type=F32)
    b = jnp.dot(c, wp_ref[...], preferred_element_type=F32)
    cos = cos_ref[...]
    sin = sin_ref[...]
    for h in range(MLA_HEADS):
        o_ref[:, h * 256:h * 256 + 128] = a[:, h * 256:h * 256 + 128].astype(BF16)
        r = a[:, h * 256 + 128:(h + 1) * 256] * cos + b[:, h * 128:(h + 1) * 128] * sin
        o_ref[:, h * 256 + 128:(h + 1) * 256] = r.astype(BF16)


def _q_proj(cqn, w2, w2p, cos_t, sin_t, tm=512):
    T = cqn.shape[0]
    row = lambda n: pl.BlockSpec((tm, n), lambda i: (i, 0))
    full = lambda a: pl.BlockSpec(a.shape, lambda i: (0,) * a.ndim)
    return pl.pallas_call(
        _q_proj_kernel,
        grid=(T // tm,),
        in_specs=[row(cqn.shape[1]), full(w2), full(w2p), row(LANES), row(LANES)],
        out_specs=row(MLA_HEADS * 256),
        out_shape=jax.ShapeDtypeStruct((T, MLA_HEADS * 256), BF16),
        compiler_params=_cparams(("parallel",)),
        name="q_proj",
    )(cqn, w2, w2p, cos_t, sin_t)


def _softmax_step(s, v, h, m_scr, l_scr, acc_scr):
    m_old = m_scr[h:h + 1, :]
    m_new = jnp.maximum(m_old, jnp.max(s, axis=0, keepdims=True))
    alpha = jnp.exp(m_old - m_new)
    p = jnp.exp(s - m_new)
    l_scr[h:h + 1, :] = alpha * l_scr[h:h + 1, :] + jnp.sum(p, axis=0, keepdims=True)
    pv = lax.dot_general(v, p.astype(BF16), TN, preferred_element_type=F32)
    acc_scr[h] = alpha * acc_scr[h] + pv
    m_scr[h:h + 1, :] = m_new


def _mla_prefill_kernel(q_ref, kv_ref, wv_ref, o_ref, m_scr, l_scr, acc_scr, ot_scr, *, scale):
    qb = pl.program_id(1)
    m_scr[...] = jnp.full(m_scr.shape, NEG, F32)
    l_scr[...] = jnp.zeros(l_scr.shape, F32)
    acc_scr[...] = jnp.zeros(acc_scr.shape, F32)
    tri = lax.broadcasted_iota(I32, (BLK, BLK), 0) <= lax.broadcasted_iota(I32, (BLK, BLK), 1)

    def block(kb, causal):
        off = pl.multiple_of(kb * BLK, BLK)
        kv = kv_ref[pl.ds(off, BLK), :]
        v = kv[:, :MLA_KV_LORA]
        for h in range(MLA_HEADS):
            qh = q_ref[:, h * 256:(h + 1) * 256]
            s = lax.dot_general(kv, qh, NT, preferred_element_type=F32) * scale
            if causal:
                s = jnp.where(tri, s, NEG)
            _softmax_step(s, v, h, m_scr, l_scr, acc_scr)

    def body(kb, c):
        block(kb, False)
        return c

    lax.fori_loop(0, qb, body, 0)
    block(qb, True)
    for h in range(MLA_HEADS):
        o_lat = (acc_scr[h] / l_scr[h:h + 1, :]).astype(BF16)
        ot_scr[h * MLA_V:(h + 1) * MLA_V, :] = jnp.dot(wv_ref[h], o_lat, preferred_element_type=F32)
    o_ref[...] = ot_scr[...].T.astype(BF16)


def _mla_prefill(q_all, kvc, wv_t, B, L):
    nq = L // BLK
    scale = (MLA_NOPE + MLA_ROPE) ** -0.5
    return pl.pallas_call(
        functools.partial(_mla_prefill_kernel, scale=scale),
        grid=(B, nq),
        in_specs=[pl.BlockSpec((BLK, q_all.shape[1]), lambda b, i: (b * nq + i, 0)),
                  pl.BlockSpec((L, kvc.shape[1]), lambda b, i: (b, 0)),
                  pl.BlockSpec(wv_t.shape, lambda b, i: (0, 0, 0))],
        out_specs=pl.BlockSpec((BLK, BRANCH_W), lambda b, i: (b * nq + i, 0)),
        out_shape=jax.ShapeDtypeStruct((B * L, BRANCH_W), BF16),
        scratch_shapes=[pltpu.VMEM((MLA_HEADS, BLK), F32), pltpu.VMEM((MLA_HEADS, BLK), F32),
                        pltpu.VMEM((MLA_HEADS, MLA_KV_LORA, BLK), F32), pltpu.VMEM((BRANCH_W, BLK), F32)],
        compiler_params=_cparams(("parallel", "arbitrary")),
        name="mla_prefill",
    )(q_all, kvc, wv_t)


def _count(keys_ref, nblk, pred):
    def body(i, acc):
        off = pl.multiple_of(i * BLK, BLK)
        c = jnp.where(pred(keys_ref[pl.ds(off, BLK), :], off), 1, 0).astype(I32)
        return acc + jnp.sum(c.reshape(BLK // 8, 8, LANES), axis=0)

    acc = lax.fori_loop(0, nblk, body, jnp.zeros((8, LANES), I32))
    return jnp.sum(acc, axis=0, keepdims=True)


def _kth_key(keys_ref, nblk, k):
    def it(i, tu):
        cand = tu | lax.shift_left(jnp.int32(1), 31 - i)
        cs = cand ^ jnp.int32(INT_MIN)
        cnt = _count(keys_ref, nblk, lambda blk, off: blk >= cs)
        return jnp.where(cnt >= k, cand, tu)

    tu = lax.fori_loop(0, 32, it, jnp.zeros((1, LANES), I32))
    return tu ^ jnp.int32(INT_MIN)


def _tie_cut(keys_ref, nblk, thr, need, nbits):
    rows = lax.broadcasted_iota(I32, (BLK, LANES), 0)

    def it(i, c):
        cand = c | lax.shift_left(jnp.int32(1), nbits - 1 - i)
        cnt = _count(keys_ref, nblk,
                     lambda blk, off: jnp.where(blk == thr, jnp.where(rows + off < cand, 1, 0), 0) > 0)
        return jnp.where(cnt <= need, cand, c)

    return lax.fori_loop(0, nbits, it, jnp.zeros((1, LANES), I32))


def _select_threshold(keys_ref, nblk, k, nbits, cut_scr):
    thr = _kth_key(keys_ref, nblk, k)
    n_ge = _count(keys_ref, nblk, lambda blk, off: blk >= thr)
    n_gt = _count(keys_ref, nblk, lambda blk, off: blk > thr)
    cut_scr[...] = jnp.full((1, LANES), 2 ** 30, I32)

    @pl.when(jnp.max(n_ge) > k)
    def _():
        cut = _tie_cut(keys_ref, nblk, thr, k - n_gt, nbits)
        cut_scr[...] = jnp.where(n_ge > k, cut, 2 ** 30)

    return thr, cut_scr[...]


def _sel_bias(key, pos, thr, cut):
    return jnp.where(key > thr, 0.0, jnp.where(key == thr, jnp.where(pos < cut, 0.0, NEG), NEG))


def _dsa_prefill_kernel(bfar_ref, qi_ref, wit_ref, ki_ref, qd_ref, kd_ref, vd_ref, d0_ref, d1_ref, o_ref,
                        keys_scr, cut_scr, m_scr, l_scr, acc_scr, ot_scr, *, n_sel, nbits):
    qb = pl.program_id(1)
    rows = lax.broadcasted_iota(I32, (BLK, BLK), 0)
    tri = rows <= lax.broadcasted_iota(I32, (BLK, BLK), 1)

    def score_block(kb, causal):
        off = pl.multiple_of(kb * BLK, BLK)
        ki = ki_ref[pl.ds(off, BLK), :].astype(BF16)
        sc = jnp.zeros((BLK, BLK), F32)
        for hi in range(IDX_HEADS):
            qh = qi_ref[:, hi * LANES:hi * LANES + IDX_DIM]
            d = lax.dot_general(ki, qh, NT, preferred_element_type=F32) * (IDX_DIM ** -0.5)
            sc = sc + wit_ref[hi:hi + 1, :] * jnp.maximum(d, 0.0)
        key = _to_key(sc)
        if causal:
            key = jnp.where(tri, key, INT_MIN)
        keys_scr[pl.ds(off, BLK), :] = key

    def sbody(kb, c):
        score_block(kb, False)
        return c

    lax.fori_loop(0, qb, sbody, 0)
    score_block(qb, True)

    thr, cut = _select_threshold(keys_scr, qb + 1, n_sel, nbits, cut_scr)

    m_scr[...] = jnp.full(m_scr.shape, NEG, F32)
    l_scr[...] = jnp.zeros(l_scr.shape, F32)
    acc_scr[...] = jnp.zeros(acc_scr.shape, F32)

    def attend(kb, bias_fn, causal):
        off = pl.multiple_of(kb * BLK, BLK)
        sel = _sel_bias(keys_scr[pl.ds(off, BLK), :], rows + off, thr, cut)
        if causal:
            sel = jnp.where(tri, sel, NEG)
        kd = kd_ref[pl.ds(off, BLK), :].astype(BF16)
        vd = vd_ref[pl.ds(off, BLK), :].astype(BF16)
        for h in range(DSA_HEADS):
            qh = qd_ref[:, h * LANES:(h + 1) * LANES]
            s = lax.dot_general(kd, qh, NT, preferred_element_type=F32) * (DSA_HEAD_DIM ** -0.5)
            s = s + bias_fn(h) + sel
            _softmax_step(s, vd, h, m_scr, l_scr, acc_scr)

    def abody(kb, c):
        attend(kb, lambda h: bfar_ref[h], False)
        return c

    lax.fori_loop(0, qb - 1, abody, 0)

    @pl.when(qb >= 1)
    def _():
        attend(qb - 1, lambda h: d1_ref[h], False)

    attend(qb, lambda h: d0_ref[h], True)
    grp = DSA_HEADS // DSA_KV_HEADS
    for h in range(DSA_HEADS):
        g = h // grp
        a = acc_scr[h, g * DSA_HEAD_DIM:(g + 1) * DSA_HEAD_DIM, :]
        ot_scr[h * DSA_HEAD_DIM:(h + 1) * DSA_HEAD_DIM, :] = a / l_scr[h:h + 1, :]
    o_ref[...] = ot_scr[...].T.astype(BF16)


def _dsa_prefill(bias_far, qi, wi_t, ki, qd, kd, vd, d0, d1, B, L, n_sel):
    nq = L // BLK
    nbits = max(1, int(L - 1).bit_length())
    tok = lambda n: pl.BlockSpec((BLK, n), lambda b, i: (b * nq + i, 0))
    seq = lambda n: pl.BlockSpec((L, n), lambda b, i: (b, 0))
    cst = lambda a: pl.BlockSpec(a.shape, lambda b, i: (0,) * a.ndim)
    return pl.pallas_call(
        functools.partial(_dsa_prefill_kernel, n_sel=n_sel, nbits=nbits),
        grid=(B, nq),
        in_specs=[pl.BlockSpec(memory_space=pltpu.SMEM),
                  tok(qi.shape[1]),
                  pl.BlockSpec((None, IDX_HEADS, BLK), lambda b, i: (b, 0, i)),
                  seq(ki.shape[1]), tok(qd.shape[1]), seq(kd.shape[1]), seq(vd.shape[1]), cst(d0), cst(d1)],
        out_specs=pl.BlockSpec((BLK, BRANCH_W), lambda b, i: (b * nq + i, 0)),
        out_shape=jax.ShapeDtypeStruct((B * L, BRANCH_W), BF16),
        scratch_shapes=[pltpu.VMEM((L, LANES), I32), pltpu.VMEM((1, LANES), I32),
                        pltpu.VMEM((DSA_HEADS, BLK), F32), pltpu.VMEM((DSA_HEADS, BLK), F32),
                        pltpu.VMEM((DSA_HEADS, LANES, BLK), F32), pltpu.VMEM((BRANCH_W, BLK), F32)],
        compiler_params=_cparams(("parallel", "arbitrary")),
        name="dsa_prefill",
    )(bias_far, qi, wi_t, ki, qd, kd, vd, d0, d1)


def _row_softmax_step(s, m_scr, l_scr):
    m_old = m_scr[...]
    m_new = jnp.maximum(m_old, jnp.max(s, axis=1, keepdims=True))
    alpha = jnp.exp(m_old - m_new)
    p = jnp.exp(s - m_new)
    l_scr[...] = alpha * l_scr[...] + jnp.sum(p, axis=1, keepdims=True)
    m_scr[...] = m_new
    return p, alpha


def _new_key_mask(nrows):
    qidx = lax.broadcasted_iota(I32, (nrows, BLK), 0) % QPAD
    return lax.broadcasted_iota(I32, (nrows, BLK), 1) <= qidx


def _mla_decode_kernel(pt_ref, q_ref, knew_ref, wv_ref, *rest, G, scale):
    ckv_pg, kr_pg = rest[:G], rest[G:2 * G]
    o_ref, m_scr, l_scr, acc_scr = rest[2 * G:]
    j = pl.program_id(1)
    R = q_ref.shape[0]

    @pl.when(j == 0)
    def _():
        m_scr[...] = jnp.full(m_scr.shape, NEG, F32)
        l_scr[...] = jnp.zeros(l_scr.shape, F32)
        acc_scr[...] = jnp.zeros(acc_scr.shape, F32)

    q = q_ref[...]
    ql, qr = q[:, :MLA_KV_LORA], q[:, MLA_KV_LORA:MLA_KV_LORA + MLA_ROPE]
    cks, ss = [], []
    for g in range(G):
        ck = ckv_pg[g][...].astype(BF16)
        kr = kr_pg[g][...].astype(BF16)
        cks.append(ck)
        ss.append(lax.dot_general(ql, ck, NT, preferred_element_type=F32)
                  + lax.dot_general(qr, kr, NT, preferred_element_type=F32))
    s = jnp.concatenate(ss, axis=1) * scale
    p, alpha = _row_softmax_step(s, m_scr, l_scr)
    p = p.astype(BF16)
    pv = jnp.zeros((R, MLA_KV_LORA), F32)
    for g in range(G):
        pv = pv + jnp.dot(p[:, g * BLK:(g + 1) * BLK], cks[g], preferred_element_type=F32)
    acc_scr[...] = alpha * acc_scr[...] + pv

    @pl.when(j == pl.num_programs(1) - 1)
    def _():
        kn = knew_ref[...]
        s2 = lax.dot_general(q, kn, NT, preferred_element_type=F32) * scale
        s2 = jnp.where(_new_key_mask(R), s2, NEG)
        p2, alpha2 = _row_softmax_step(s2, m_scr, l_scr)
        acc = alpha2 * acc_scr[...] + jnp.dot(p2.astype(BF16), kn[:, :MLA_KV_LORA], preferred_element_type=F32)
        o_lat = (acc / l_scr[...]).astype(BF16)
        for h in range(MLA_HEADS):
            o_ref[h * QPAD:(h + 1) * QPAD, :] = jnp.dot(o_lat[h * QPAD:(h + 1) * QPAD, :], wv_ref[h],
                                                        preferred_element_type=F32)


def _page_specs(shape_tail, layer, G):
    def spec(g):
        return pl.BlockSpec((None, None) + shape_tail,
                            lambda b, j, pt: (layer, pt[b, j * G + g]) + (0,) * len(shape_tail))
    return [spec(g) for g in range(G)]


def _mla_decode(page_table, q_dec, kv_new, wv, pool_ckv, pool_kr, layer):
    DB, n_pages = page_table.shape
    G = DEC_PAGES_PER_STEP
    R = q_dec.shape[1]
    scale = (MLA_NOPE + MLA_ROPE) ** -0.5
    per_b = lambda a: pl.BlockSpec((None,) + a.shape[1:], lambda b, j, pt: (b,) + (0,) * (a.ndim - 1))
    grid_spec = pltpu.PrefetchScalarGridSpec(
        num_scalar_prefetch=1,
        grid=(DB, n_pages // G),
        in_specs=[per_b(q_dec), per_b(kv_new), pl.BlockSpec(wv.shape, lambda b, j, pt: (0, 0, 0))]
        + _page_specs(pool_ckv.shape[2:], layer, G) + _page_specs(pool_kr.shape[2:], layer, G),
        out_specs=pl.BlockSpec((None, R, MLA_V), lambda b, j, pt: (b, 0, 0)),
        scratch_shapes=[pltpu.VMEM((R, 1), F32), pltpu.VMEM((R, 1), F32), pltpu.VMEM((R, MLA_KV_LORA), F32)],
    )
    return pl.pallas_call(
        functools.partial(_mla_decode_kernel, G=G, scale=scale),
        grid_spec=grid_spec,
        out_shape=jax.ShapeDtypeStruct((DB, R, MLA_V), F32),
        compiler_params=_cparams(("parallel", "arbitrary")),
        name="mla_decode",
    )(page_table, q_dec, kv_new, wv, *([pool_ckv] * G), *([pool_kr] * G))


def _idx_rows(qi, wi, ki):
    d = lax.dot_general(qi, ki, NT, preferred_element_type=F32) * (IDX_DIM ** -0.5)
    r = wi * jnp.maximum(d, 0.0)
    sc = r[0:QPAD]
    for hi in range(1, IDX_HEADS):
        sc = sc + r[hi * QPAD:(hi + 1) * QPAD]
    return sc


def _idx_decode_kernel(pt_ref, qi_ref, wi_ref, kinew_ref, *rest, G):
    pages = rest[:G]
    past_o, new_o = rest[G:]
    qi = qi_ref[...]
    wi = wi_ref[...]
    for g in range(G):
        past_o[:, g * BLK:(g + 1) * BLK] = _idx_rows(qi, wi, pages[g][...].astype(BF16))
    sn = _idx_rows(qi, wi, kinew_ref[...].astype(BF16))
    new_o[...] = jnp.where(_new_key_mask(QPAD), sn, -jnp.inf)


def _idx_decode(page_table, qi_dec, wi_dec, ki_new, pool_ki, layer):
    DB, n_pages = page_table.shape
    G = DEC_PAGES_PER_STEP
    per_b = lambda a: pl.BlockSpec((None,) + a.shape[1:], lambda b, j, pt: (b,) + (0,) * (a.ndim - 1))
    grid_spec = pltpu.PrefetchScalarGridSpec(
        num_scalar_prefetch=1,
        grid=(DB, n_pages // G),
        in_specs=[per_b(qi_dec), per_b(wi_dec), per_b(ki_new)] + _page_specs(pool_ki.shape[2:], layer, G),
        out_specs=[pl.BlockSpec((None, QPAD, G * BLK), lambda b, j, pt: (b, 0, j)),
                   pl.BlockSpec((None, QPAD, BLK), lambda b, j, pt: (b, 0, 0))],
    )
    return pl.pallas_call(
        functools.partial(_idx_decode_kernel, G=G),
        grid_spec=grid_spec,
        out_shape=[jax.ShapeDtypeStruct((DB, QPAD, n_pages * BLK), F32),
                   jax.ShapeDtypeStruct((DB, QPAD, BLK), F32)],
        compiler_params=_cparams(("parallel", "arbitrary")),
        name="idx_decode",
    )(page_table, qi_dec, wi_dec, ki_new, *([pool_ki] * G))


def _thresh_kernel(s_ref, thr_o, cut_o, keys_scr, cut_scr, *, n_sel, nbits):
    nblk = s_ref.shape[0] // BLK

    def body(i, c):
        off = pl.multiple_of(i * BLK, BLK)
        keys_scr[pl.ds(off, BLK), :] = _to_key(s_ref[pl.ds(off, BLK), :])
        return c

    lax.fori_loop(0, nblk, body, 0)
    thr, cut = _select_threshold(keys_scr, nblk, n_sel, nbits, cut_scr)
    thr_o[...] = thr
    cut_o[...] = cut


def _thresh(scores_t, n_sel):
    S, Q = scores_t.shape
    nbits = max(1, int(S - 1).bit_length())
    return pl.pallas_call(
        functools.partial(_thresh_kernel, n_sel=n_sel, nbits=nbits),
        grid=(Q // LANES,),
        in_specs=[pl.BlockSpec((S, LANES), lambda i: (0, i))],
        out_specs=[pl.BlockSpec((1, LANES), lambda i: (0, i)), pl.BlockSpec((1, LANES), lambda i: (0, i))],
        out_shape=[jax.ShapeDtypeStruct((1, Q), I32), jax.ShapeDtypeStruct((1, Q), I32)],
        scratch_shapes=[pltpu.VMEM((S, LANES), I32), pltpu.VMEM((1, LANES), I32)],
        compiler_params=_cparams(("parallel",)),
        name="topk_thresh",
    )(scores_t)


def _dsa_decode_kernel(pt_ref, q_ref, sc_ref, thr_ref, cut_ref, bias_ref, knew_ref, vnew_ref, scnew_ref,
                       biasnew_ref, *rest, G, past_len):
    k_pg, v_pg = rest[:G], rest[G:2 * G]
    o_ref, m_scr, l_scr, acc_scr = rest[2 * G:]
    j = pl.program_id(1)
    R = q_ref.shape[0]
    reps = R // QPAD

    @pl.when(j == 0)
    def _():
        m_scr[...] = jnp.full(m_scr.shape, NEG, F32)
        l_scr[...] = jnp.zeros(l_scr.shape, F32)
        acc_scr[...] = jnp.zeros(acc_scr.shape, F32)

    q = q_ref[...]
    thr = thr_ref[...]
    cut = cut_ref[...]

    def sel_rows(sc, pos):
        sel = _sel_bias(_to_key(sc), pos, thr, cut)
        return jnp.concatenate([sel] * reps, axis=0)

    W = G * BLK
    pos = j * W + lax.broadcasted_iota(I32, (QPAD, W), 1)
    ss = [lax.dot_general(q, k_pg[g][...].astype(BF16), NT, preferred_element_type=F32) for g in range(G)]
    s = jnp.concatenate(ss, axis=1) * (DSA_HEAD_DIM ** -0.5) + bias_ref[...] + sel_rows(sc_ref[...], pos)
    p, alpha = _row_softmax_step(s, m_scr, l_scr)
    p = p.astype(BF16)
    pv = jnp.zeros((R, LANES), F32)
    for g in range(G):
        pv = pv + jnp.dot(p[:, g * BLK:(g + 1) * BLK], v_pg[g][...].astype(BF16), preferred_element_type=F32)
    acc_scr[...] = alpha * acc_scr[...] + pv

    @pl.when(j == pl.num_programs(1) - 1)
    def _():
        kn = knew_ref[...].astype(BF16)
        pos2 = past_len + lax.broadcasted_iota(I32, (QPAD, BLK), 1)
        s2 = (lax.dot_general(q, kn, NT, preferred_element_type=F32) * (DSA_HEAD_DIM ** -0.5)
              + biasnew_ref[...] + sel_rows(scnew_ref[...], pos2))
        s2 = jnp.where(_new_key_mask(R), s2, NEG)
        p2, alpha2 = _row_softmax_step(s2, m_scr, l_scr)
        acc = alpha2 * acc_scr[...] + jnp.dot(p2.astype(BF16), vnew_ref[...].astype(BF16),
                                              preferred_element_type=F32)
        o = acc / l_scr[...]
        half = R // DSA_KV_HEADS
        for g in range(DSA_KV_HEADS):
            o_ref[g * half:(g + 1) * half, :] = o[g * half:(g + 1) * half, g * DSA_HEAD_DIM:(g + 1) * DSA_HEAD_DIM]


def _dsa_decode(page_table, qd_dec, sc_past, thr, cut, bias_past, k_new, v_new, sc_new, bias_new,
                pool_k, pool_v, layer):
    DB, n_pages = page_table.shape
    G = DEC_PAGES_PER_STEP
    R = qd_dec.shape[1]
    W = G * BLK
    per_b = lambda a: pl.BlockSpec((None,) + a.shape[1:], lambda b, j, pt: (b,) + (0,) * (a.ndim - 1))
    grid_spec = pltpu.PrefetchScalarGridSpec(
        num_scalar_prefetch=1,
        grid=(DB, n_pages // G),
        in_specs=[per_b(qd_dec),
                  pl.BlockSpec((None, QPAD, W), lambda b, j, pt: (b, 0, j)),
                  per_b(thr), per_b(cut),
                  pl.BlockSpec((R, W), lambda b, j, pt: (0, j)),
                  per_b(k_new), per_b(v_new), per_b(sc_new),
                  pl.BlockSpec(bias_new.shape, lambda b, j, pt: (0, 0))]
        + _page_specs(pool_k.shape[2:], layer, G) + _page_specs(pool_v.shape[2:], layer, G),
        out_specs=pl.BlockSpec((None, R, DSA_HEAD_DIM), lambda b, j, pt: (b, 0, 0)),
        scratch_shapes=[pltpu.VMEM((R, 1), F32), pltpu.VMEM((R, 1), F32), pltpu.VMEM((R, LANES), F32)],
    )
    return pl.pallas_call(
        functools.partial(_dsa_decode_kernel, G=G, past_len=n_pages * BLK),
        grid_spec=grid_spec,
        out_shape=jax.ShapeDtypeStruct((DB, R, DSA_HEAD_DIM), F32),
        compiler_params=_cparams(("parallel", "arbitrary")),
        name="dsa_decode",
    )(page_table, qd_dec, sc_past, thr, cut, bias_past, k_new, v_new, sc_new, bias_new,
      *([pool_k] * G), *([pool_v] * G))


def _ssd_kernel(z_ref, xbc_ref, dt_ref, convp_ref, ssmp_ref, cw_ref, cb_ref, dtb_ref, alog_ref, dpar_ref, nw_ref,
                y_ref, convn_ref, ssmn_ref, xe_scr, st_scr, y_scr, *, Q, Lr):
    c = pl.program_id(1)
    PADR = 8
    W = SSD_CONV

    @pl.when(c == 0)
    def _():
        st_scr[...] = ssmp_ref[...]
        xe_scr[PADR - (W - 1):PADR, :] = convp_ref[...]

    xe_scr[PADR:PADR + Q, :] = xbc_ref[...]
    conv = cb_ref[...]
    for k in range(W):
        conv = conv + xe_scr[PADR - (W - 1) + k:PADR - (W - 1) + k + Q, :] * cw_ref[k:k + 1, :]
    tail = xe_scr[PADR + Lr - (W - 1):PADR + Lr, :]
    xe_scr[PADR - (W - 1):PADR, :] = tail
    convn_ref[...] = tail
    u = conv * _sigmoid(conv)
    xs = u[:, :SSD_INNER]
    GS = SSD_GROUPS * SSD_STATE
    bm = u[:, SSD_INNER:SSD_INNER + GS].astype(BF16)
    cm = u[:, SSD_INNER + GS:].astype(BF16)

    x = dt_ref[...] + dtb_ref[...]
    dt = jnp.maximum(x, 0.0) + jnp.log1p(jnp.exp(-jnp.abs(x)))
    if Lr < Q:
        dt = jnp.where(lax.broadcasted_iota(I32, dt.shape, 0) < Lr, dt, 0.0)
    ad = dt * (-jnp.exp(alog_ref[...]))
    ri = lax.broadcasted_iota(I32, (Q, Q), 0)
    ci = lax.broadcasted_iota(I32, (Q, Q), 1)
    low = ri >= ci
    a_cum = jnp.dot(low.astype(F32), ad, precision=lax.Precision.HIGHEST, preferred_element_type=F32)
    a_cum_t = lax.dot_general(ad, (ri <= ci).astype(F32), TN, precision=lax.Precision.HIGHEST,
                              preferred_element_type=F32)
    a_last = a_cum[Q - 1:Q, :]
    hpg = SSD_HEADS // SSD_GROUPS
    P, N = SSD_HEAD_DIM, SSD_STATE
    for g in range(SSD_GROUPS):
        bg = bm[:, g * N:(g + 1) * N]
        cg = cm[:, g * N:(g + 1) * N]
        cb = lax.dot_general(cg, bg, NT, preferred_element_type=F32)
        for h in range(g * hpg, (g + 1) * hpg):
            ac = a_cum[:, h:h + 1]
            lmat = jnp.where(low, jnp.exp(jnp.minimum(ac - a_cum_t[h:h + 1, :], 0.0)), 0.0)
            xh = xs[:, h * P:(h + 1) * P]
            xdt = xh * dt[:, h:h + 1]
            st = st_scr[h]
            y = jnp.dot((cb * lmat).astype(BF16), xdt.astype(BF16), preferred_element_type=F32)
            y = y + lax.dot_general(cg, st.astype(BF16), NT, preferred_element_type=F32) * jnp.exp(ac)
            y_scr[:, h * P:(h + 1) * P] = y + dpar_ref[:, h:h + 1] * xh
            dec = jnp.exp(a_last[:, h:h + 1] - ac)
            st_scr[h] = (jnp.exp(a_last[:, h:h + 1]) * st
                         + lax.dot_general((xdt * dec).astype(BF16), bg, TN, preferred_element_type=F32))
    zz = z_ref[...]
    y_ref[...] = _rms(y_scr[...] * (zz * _sigmoid(zz)), nw_ref[...]).astype(BF16)
    ssmn_ref[...] = st_scr[...]


def _ssd(z, xbc, dt, conv_prev, ssm_prev, cw, cb, dtb, alog, dpar, nw, nseq, nchunks, Q, Lr):
    tok = lambda n: pl.BlockSpec((Q, n), lambda s, c: (s * nchunks + c, 0))
    cst = lambda a: pl.BlockSpec(a.shape, lambda s, c: (0,) * a.ndim)
    per_s = lambda shp: pl.BlockSpec((None,) + shp, lambda s, c: (s,) + (0,) * len(shp))
    W1 = SSD_CONV - 1
    return pl.pallas_call(
        functools.partial(_ssd_kernel, Q=Q, Lr=Lr),
        grid=(nseq, nchunks),
        in_specs=[tok(SSD_INNER), tok(SSD_CONV_DIM), tok(SSD_HEADS), per_s((W1, SSD_CONV_DIM)),
                  per_s((SSD_HEADS, SSD_HEAD_DIM, SSD_STATE)), cst(cw), cst(cb), cst(dtb), cst(alog), cst(dpar),
                  cst(nw)],
        out_specs=[tok(SSD_INNER), per_s((W1, SSD_CONV_DIM)), per_s((SSD_HEADS, SSD_HEAD_DIM, SSD_STATE))],
        out_shape=[jax.ShapeDtypeStruct((nseq * nchunks * Q, SSD_INNER), BF16),
                   jax.ShapeDtypeStruct((nseq, W1, SSD_CONV_DIM), F32),
                   jax.ShapeDtypeStruct((nseq, SSD_HEADS, SSD_HEAD_DIM, SSD_STATE), F32)],
        scratch_shapes=[pltpu.VMEM((8 + Q, SSD_CONV_DIM), F32),
                        pltpu.VMEM((SSD_HEADS, SSD_HEAD_DIM, SSD_STATE), F32),
                        pltpu.VMEM((Q, SSD_INNER), F32)],
        compiler_params=_cparams(("parallel", "arbitrary")),
        name="ssd",
    )(z, xbc, dt, conv_prev, ssm_prev, cw, cb, dtb, alog, dpar, nw)


def _merge_kernel(x_ref, oa_ref, ob_ref, oc_ref, g_ref, gb_ref, wb_ref, wo_ref, o_ref):
    D = x_ref.shape[1]
    merged = jnp.zeros(x_ref.shape, F32)
    for n, o in enumerate((oa_ref, ob_ref, oc_ref)):
        up = jnp.dot(o[...], wb_ref[n], preferred_element_type=F32)
        merged = merged + _sigmoid(g_ref[:, n * D:(n + 1) * D] + gb_ref[n:n + 1, :]) * up
    o_ref[...] = x_ref[...] + jnp.dot(merged.astype(BF16), wo_ref[...], preferred_element_type=F32)


def _merge(x, oa, ob, oc, gate_raw, gate_b, w_branch, w_out, tm=512):
    T, D = x.shape
    row = lambda n: pl.BlockSpec((tm, n), lambda i: (i, 0))
    full = lambda a: pl.BlockSpec(a.shape, lambda i: (0,) * a.ndim)
    return pl.pallas_call(
        _merge_kernel,
        grid=(T // tm,),
        in_specs=[row(D), row(BRANCH_W), row(BRANCH_W), row(BRANCH_W), row(N_BRANCH * D), full(gate_b),
                  full(w_branch), full(w_out)],
        out_specs=row(D),
        out_shape=jax.ShapeDtypeStruct((T, D), F32),
        compiler_params=_cparams(("parallel",)),
        name="merge_out",
    )(x, oa, ob, oc, gate_raw, gate_b, w_branch, w_out)


def _router_kernel(x_ref, nw_ref, wr_ref, br_ref, o_ref):
    h = _rms(x_ref[...], nw_ref[...])
    lg = lax.dot_general(wr_ref[...], h, NT, precision=lax.Precision.HIGHEST,
                         preferred_element_type=F32) + br_ref[...]
    E = lg.shape[0]
    ie = lax.broadcasted_iota(I32, lg.shape, 0)
    m1 = jnp.max(lg, axis=0, keepdims=True)
    i1 = jnp.min(jnp.where(lg == m1, ie, E), axis=0, keepdims=True)
    l2 = jnp.where(ie == i1, -jnp.inf, lg)
    m2 = jnp.max(l2, axis=0, keepdims=True)
    i2 = jnp.min(jnp.where(l2 == m2, ie, E), axis=0, keepdims=True)
    e2 = jnp.exp(m2 - m1)
    den = 1.0 + e2
    o_ref[...] = jnp.where(ie == i1, 1.0 / den, jnp.where(ie == i2, e2 / den, 0.0))


def _router(x, nw, wr_t, br, tm=512):
    T, D = x.shape
    E = wr_t.shape[0]
    full = lambda a: pl.BlockSpec(a.shape, lambda i: (0,) * a.ndim)
    return pl.pallas_call(
        _router_kernel,
        grid=(T // tm,),
        in_specs=[pl.BlockSpec((tm, D), lambda i: (i, 0)), full(nw), full(wr_t), full(br)],
        out_specs=pl.BlockSpec((E, tm), lambda i: (0, i)),
        out_shape=jax.ShapeDtypeStruct((E, T), F32),
        compiler_params=_cparams(("parallel",)),
        name="router",
    )(x, nw, wr_t, br)


def _ffn_kernel(x_ref, nw_ref, gate_ref, wg_ref, wu_ref, wd_ref, fnw_ref, o_ref, h_scr, acc_scr, *, final_norm):
    e = pl.program_id(1)

    @pl.when(e == 0)
    def _():
        h_scr[...] = _rms(x_ref[...], nw_ref[...]).astype(BF16)
        acc_scr[...] = jnp.zeros(acc_scr.shape, F32)

    h = h_scr[...]
    g = jnp.dot(h, wg_ref[...], preferred_element_type=F32)
    u = jnp.dot(h, wu_ref[...], preferred_element_type=F32)
    a = (g * _sigmoid(g) * u).astype(BF16)
    acc_scr[...] += gate_ref[...] * jnp.dot(a, wd_ref[...], preferred_element_type=F32)

    @pl.when(e == pl.num_programs(1) - 1)
    def _():
        y = x_ref[...] + acc_scr[...]
        if final_norm:
            y = _rms(y, fnw_ref[...])
        o_ref[...] = y


def _ffn(x, nw, gates, wg, wu, wd, fnw, final_norm, tm=512):
    T, D = x.shape
    E, _, Fe = wg.shape
    full = lambda a: pl.BlockSpec(a.shape, lambda i, e: (0,) * a.ndim)
    return pl.pallas_call(
        functools.partial(_ffn_kernel, final_norm=final_norm),
        grid=(T // tm, E),
        in_specs=[pl.BlockSpec((tm, D), lambda i, e: (i, 0)), full(nw),
                  pl.BlockSpec((None, tm, 1), lambda i, e: (e, i, 0)),
                  pl.BlockSpec((None, D, Fe), lambda i, e: (e, 0, 0)),
                  pl.BlockSpec((None, D, Fe), lambda i, e: (e, 0, 0)),
                  pl.BlockSpec((None, Fe, D), lambda i, e: (e, 0, 0)), full(fnw)],
        out_specs=pl.BlockSpec((tm, D), lambda i, e: (i, 0)),
        out_shape=jax.ShapeDtypeStruct((T, D), F32),
        scratch_shapes=[pltpu.VMEM((tm, D), BF16), pltpu.VMEM((tm, D), F32)],
        compiler_params=_cparams(("parallel", "arbitrary")),
        name="ffn",
    )(x, nw, gates, wg, wu, wd, fnw)


def _rel_bucket(dist):
    n = jnp.maximum(dist, 0)
    nf = jnp.maximum(n, 1).astype(F32)
    large = REL_MAX_EXACT + (jnp.log(nf / REL_MAX_EXACT) / math.log(REL_MAX_DIST / REL_MAX_EXACT)
                             * (REL_BUCKETS - REL_MAX_EXACT)).astype(I32)
    large = jnp.minimum(large, REL_BUCKETS - 1)
    return jnp.where(n < REL_MAX_EXACT, n, large)


def _rope_tables(pos):
    half = MLA_ROPE // 2
    freqs = ROPE_THETA ** (-jnp.arange(half, dtype=F32) / half)
    ang = pos.astype(F32)[:, None] * freqs
    pad = jnp.zeros((pos.shape[0], LANES - MLA_ROPE), F32)
    cos = jnp.concatenate([jnp.cos(ang), jnp.cos(ang), pad], axis=1)
    sin = jnp.concatenate([jnp.sin(ang), jnp.sin(ang), pad], axis=1)
    return cos, sin


def _rot_cols(w):
    half = w.shape[-1] // 2
    return jnp.concatenate([-w[..., half:], w[..., :half]], axis=-1)


def _pad_cols(w, n):
    return jnp.pad(w, ((0, 0), (0, n - w.shape[1])))


def _prep_w_in(w):
    D = w.shape[0]
    sizes = (MLA_Q_LORA, MLA_KV_LORA, MLA_ROPE, DSA_HEADS * DSA_HEAD_DIM, DSA_KV_HEADS * DSA_HEAD_DIM,
             DSA_KV_HEADS * DSA_HEAD_DIM, IDX_HEADS * IDX_DIM, IDX_DIM, IDX_HEADS, SSD_INNER, SSD_CONV_DIM,
             SSD_HEADS, N_BRANCH * D)
    cuts = np.cumsum(sizes)[:-1].tolist()
    cq, ckv, kr, qd, kd, vd, qi, ki, wi, z, xbc, dt, gate = jnp.split(w, cuts, axis=1)
    grp = DSA_HEADS // DSA_KV_HEADS
    qd_h = qd.reshape(D, DSA_HEADS, DSA_HEAD_DIM)
    qd_bd = jnp.zeros((D, DSA_HEADS, DSA_KV_HEADS, DSA_HEAD_DIM), w.dtype)
    for h in range(DSA_HEADS):
        qd_bd = qd_bd.at[:, h, h // grp, :].set(qd_h[:, h, :])
    qi_p = jnp.pad(qi.reshape(D, IDX_HEADS, IDX_DIM), ((0, 0), (0, 0), (0, LANES - IDX_DIM)))
    parts = {
        "gate": gate, "z": z, "xbc": xbc, "qd": qd_bd.reshape(D, -1), "cq": cq, "qi": qi_p.reshape(D, -1),
        "kvc": jnp.concatenate([ckv, _pad_cols(kr, LANES)], axis=1), "kvcp": _pad_cols(_rot_cols(kr), LANES),
        "kd": kd, "vd": vd, "ki": _pad_cols(ki, LANES), "wi": _pad_cols(wi, LANES), "dt": _pad_cols(dt, LANES),
    }
    cat = jnp.concatenate([parts[n] for n, _ in IN_GROUPS], axis=1)
    assert cat.shape[1] == IN_TOTAL
    return cat.astype(BF16)


def _prep_q_weights(w_q_up, w_kv_up):
    R = w_q_up.shape[0]
    wq = w_q_up.reshape(R, MLA_HEADS, MLA_NOPE + MLA_ROPE)
    wkv = w_kv_up.reshape(MLA_KV_LORA, MLA_HEADS, MLA_NOPE + MLA_V)
    wq_nope = jnp.transpose(wq[:, :, :MLA_NOPE], (1, 0, 2))
    wk = jnp.transpose(wkv[:, :, :MLA_NOPE], (1, 0, 2))
    w_lat = jnp.transpose(_fold_q_lat(wq_nope, wk), (1, 0, 2))
    w_rope = wq[:, :, MLA_NOPE:]
    zpad = jnp.zeros((R, MLA_HEADS, 256 - MLA_KV_LORA - MLA_ROPE), F32)
    w2 = jnp.concatenate([w_lat, w_rope, zpad], axis=2).reshape(R, MLA_HEADS * 256)
    w2p = jnp.pad(_rot_cols(w_rope), ((0, 0), (0, 0), (0, LANES - MLA_ROPE))).reshape(R, MLA_HEADS * LANES)
    wv = jnp.transpose(wkv[:, :, MLA_NOPE:], (1, 0, 2))
    return w2.astype(BF16), w2p.astype(BF16), wv.astype(BF16), jnp.transpose(wv, (0, 2, 1)).astype(BF16)


def _dec_rows(a, DB, Lq, heads):
    n = a.shape[1] // heads
    a = jnp.transpose(a.reshape(DB, Lq, heads, n), (0, 2, 1, 3))
    a = jnp.pad(a, ((0, 0), (0, 0), (0, QPAD - Lq), (0, 0)))
    return a.reshape(DB, heads * QPAD, n)


def _undec_rows(o, DB, Lq, heads):
    n = o.shape[2]
    o = o.reshape(DB, heads, QPAD, n)[:, :, :Lq]
    return jnp.transpose(o, (0, 2, 1, 3)).reshape(DB * Lq, heads * n)


def _pad_new(a, DB, Lq):
    return jnp.pad(a.reshape(DB, Lq, a.shape[1]), ((0, 0), (0, BLK - Lq), (0, 0)))


def kernel(x_prompt, x_sample, cache_mla_ckv, cache_mla_krope, cache_dsa_k, cache_dsa_v, cache_dsa_idx_k, state_ssm, state_conv, page_table, rel_bias, norm_mix_w, w_in, mla_q_norm_w, mla_w_q_up, mla_kv_norm_w, mla_w_kv_up, ssd_conv_w, ssd_conv_b, ssd_dt_bias, ssd_a_log, ssd_d, ssd_norm_w, gate_b, w_branch, w_out, norm_ffn_w, ffn_w_gate, ffn_w_up, ffn_w_down, moe_router_w, moe_router_b, moe_w_gate, moe_w_up, moe_w_down, final_norm_w):
    B, L, D = x_prompt.shape
    DB, Lq, _ = x_sample.shape
    depth = w_in.shape[0]
    n_pages = page_table.shape[1]
    page = cache_mla_ckv.shape[2]
    past_len = n_pages * page
    assert page == BLK and L % BLK == 0 and Lq <= QPAD and n_pages % DEC_PAGES_PER_STEP == 0
    TP, TS = B * L, DB * Lq
    T = TP + TS
    assert T % 512 == 0

    x = jnp.concatenate([x_prompt.reshape(TP, D), x_sample.reshape(TS, D)], axis=0)
    pos = jnp.concatenate([jnp.tile(jnp.arange(L, dtype=I32), B),
                           jnp.tile(past_len + jnp.arange(Lq, dtype=I32), DB)])
    cos_t, sin_t = _rope_tables(pos)

    grp = DSA_HEADS // DSA_KV_HEADS
    tab = rel_bias[_rel_bucket(jnp.arange(2 * BLK, dtype=I32))]
    kk = jnp.arange(BLK, dtype=I32)[:, None]
    qq = jnp.arange(BLK, dtype=I32)[None, :]
    d0 = jnp.transpose(tab[jnp.clip(qq - kk, 0, 2 * BLK - 1)], (2, 0, 1))
    d1 = jnp.transpose(tab[BLK + qq - kk], (2, 0, 1))
    bias_far = tab[2 * BLK - 1]
    qd_idx = jnp.arange(QPAD, dtype=I32)
    dist_past = (past_len + qd_idx)[:, None] - jnp.arange(past_len, dtype=I32)[None, :]
    bias_past = jnp.transpose(rel_bias[_rel_bucket(dist_past)], (2, 0, 1)).reshape(DSA_HEADS * QPAD, past_len)
    dist_new = qd_idx[:, None] - jnp.arange(BLK, dtype=I32)[None, :]
    bias_new = jnp.transpose(rel_bias[_rel_bucket(dist_new)], (2, 0, 1)).reshape(DSA_HEADS * QPAD, BLK)

    n_sel_p = max(1, min(IDX_TOPK, L // 4))
    n_sel_s = max(1, min(IDX_TOPK, (past_len + Lq) // 4))
    pool_k = cache_dsa_k.reshape(cache_dsa_k.shape[:3] + (-1,))
    pool_v = cache_dsa_v.reshape(cache_dsa_v.shape[:3] + (-1,))
    zero_conv = jnp.zeros((B, SSD_CONV - 1, SSD_CONV_DIM), F32)
    zero_ssm = jnp.zeros((B, SSD_HEADS, SSD_HEAD_DIM, SSD_STATE), F32)
    row2 = lambda v: v.reshape(1, -1)

    news_p, news_s = [], []
    for l in range(depth):
        w_cat = _prep_w_in(w_in[l])
        w2, w2p, wv, wv_t = _prep_q_weights(mla_w_q_up[l], mla_w_kv_up[l])
        (gate_raw, z, xbc, qd, cqn, qi, kvc, ckv, kr, kd, vd, ki, wi, dt) = _in_proj(
            x, row2(norm_mix_w[l]), w_cat, cos_t, sin_t, row2(mla_q_norm_w[l]), row2(mla_kv_norm_w[l]))
        q_all = _q_proj(cqn, w2, w2p, cos_t, sin_t)

        oa_p = _mla_prefill(q_all, kvc, wv_t, B, L)
        wi_t = jnp.transpose(wi[:TP].reshape(B, L, IDX_HEADS), (0, 2, 1))
        ob_p = _dsa_prefill(bias_far, qi, wi_t, ki, qd, kd, vd, d0, d1, B, L, n_sel_p)
        ssd_w = (ssd_conv_w[l], row2(ssd_conv_b[l]), row2(ssd_dt_bias[l]), row2(ssd_a_log[l]), row2(ssd_d[l]),
                 row2(ssd_norm_w[l]))
        oc_p, conv_p, ssm_p = _ssd(z, xbc, dt, zero_conv, zero_ssm, *ssd_w, nseq=B, nchunks=L // BLK, Q=BLK, Lr=BLK)

        sl = lambda a: a[TP:]
        q_dec = _dec_rows(sl(q_all), DB, Lq, MLA_HEADS)
        oa_s = _undec_rows(_mla_decode(page_table, q_dec, _pad_new(sl(kvc), DB, Lq), wv,
                                       cache_mla_ckv, cache_mla_krope, l), DB, Lq, MLA_HEADS)
        qi_dec = _dec_rows(sl(qi), DB, Lq, IDX_HEADS)[:, :, :IDX_DIM]
        wi_dec = _dec_rows(sl(wi), DB, Lq, IDX_HEADS)
        sc_past, sc_new = _idx_decode(page_table, qi_dec, wi_dec, _pad_new(sl(ki), DB, Lq), cache_dsa_idx_k, l)
        sc_t = jnp.transpose(jnp.concatenate([sc_past, sc_new], axis=2)[:, :Lq].reshape(TS, -1))
        thr, cut = _thresh(sc_t, n_sel_s)
        to_b = lambda v: jnp.pad(v.reshape(DB, Lq), ((0, 0), (0, QPAD - Lq))).reshape(DB, QPAD, 1)
        ob_s = _undec_rows(_dsa_decode(page_table, _dec_rows(sl(qd), DB, Lq, DSA_HEADS), sc_past, to_b(thr),
                                       to_b(cut), bias_past, _pad_new(sl(kd), DB, Lq), _pad_new(sl(vd), DB, Lq),
                                       sc_new, bias_new, pool_k, pool_v, l), DB, Lq, DSA_HEADS)
        pad_q = lambda a: jnp.pad(a.reshape(DB, Lq, -1), ((0, 0), (0, QPAD - Lq), (0, 0))).reshape(DB * QPAD, -1)
        oc_s, conv_s, ssm_s = _ssd(pad_q(sl(z)), pad_q(sl(xbc)), pad_q(sl(dt)), state_conv[l], state_ssm[l],
                                   *ssd_w, nseq=DB, nchunks=1, Q=QPAD, Lr=Lq)
        oc_s = oc_s.reshape(DB, QPAD, -1)[:, :Lq].reshape(TS, -1)

        oa = jnp.concatenate([oa_p, oa_s.astype(BF16)], axis=0)
        ob = jnp.concatenate([ob_p, ob_s.astype(BF16)], axis=0)
        oc = jnp.concatenate([oc_p, oc_s], axis=0)
        x = _merge(x, oa, ob, oc, gate_raw, gate_b[l], w_branch[l].astype(BF16), w_out[l].astype(BF16))

        j = l // 2
        last = l == depth - 1
        if l % 2 == 0:
            Fh = ffn_w_gate.shape[2] // 2
            split = lambda w: jnp.transpose(w.reshape(D, 2, Fh), (1, 0, 2)).astype(BF16)
            gates = jnp.ones((2, T, 1), F32)
            x = _ffn(x, row2(norm_ffn_w[l]), gates, split(ffn_w_gate[j]), split(ffn_w_up[j]),
                     ffn_w_down[j].reshape(2, Fh, D).astype(BF16), row2(final_norm_w), last)
        else:
            gates = _router(x, row2(norm_ffn_w[l]), jnp.transpose(moe_router_w[j]), moe_router_b[j].reshape(-1, 1))
            x = _ffn(x, row2(norm_ffn_w[l]), gates.reshape(gates.shape + (1,)), moe_w_gate[j].astype(BF16),
                     moe_w_up[j].astype(BF16), moe_w_down[j].astype(BF16), row2(final_norm_w), last)

        rp = lambda a: a[:TP].reshape((B, L) + a.shape[1:])
        rs = lambda a: a[TP:].reshape((DB, Lq) + a.shape[1:])
        kv4 = lambda a: a.reshape(a.shape[:2] + (DSA_KV_HEADS, DSA_HEAD_DIM))
        news_p.append((rp(ckv), rp(kr), kv4(rp(kd)), kv4(rp(vd)), rp(ki), ssm_p, conv_p))
        news_s.append((rs(ckv), rs(kr), kv4(rs(kd)), kv4(rs(vd)), rs(ki), ssm_s, conv_s))

    y_prompt = x[:TP].reshape(B, L, D)
    y_sample = x[TP:].reshape(DB, Lq, D)
    stack = lambda news: [jnp.stack([nw[i] for nw in news]) for i in range(7)]
    return (y_prompt, y_sample, *stack(news_p), *stack(news_s))
```
